```python
import math
import jax, jax.numpy as jnp
from jax import lax
import numpy as np

D_MODEL = 1024
BATCH = 8
SEQ = 2048
DEPTH = 4
DEC_BATCH = 16
DEC_SEQ = 2048
PAST_LEN = 128

D_FF = 2816
CONF_WIDTH = D_MODEL
CONF_KERNEL = 31
SSD_EXPAND = 2
SSD_INNER = SSD_EXPAND * D_MODEL
SSD_HEAD_DIM = 64
SSD_HEADS = SSD_INNER // SSD_HEAD_DIM
SSD_GROUPS = 8
SSD_STATE = 128
SSD_CONV = 5
SSD_CHUNK = 128
SSD_CONV_DIM = SSD_INNER + 2 * SSD_GROUPS * SSD_STATE
N_IN = 2 * CONF_WIDTH + SSD_INNER + SSD_CONV_DIM + 2 * SSD_HEADS + 2 * D_MODEL
EPS = 1e-6

kernel_name = "hybrid_conformer_ssd_gated_encoder"


def _rms_norm(x, g):
    xf = x.astype(jnp.float32)
    y = xf * lax.rsqrt(jnp.mean(xf * xf, axis=-1, keepdims=True) + EPS)
    return (y * g.astype(jnp.float32)).astype(x.dtype)


def _layer_norm(x, g, b):
    xf = x.astype(jnp.float32)
    mu = jnp.mean(xf, axis=-1, keepdims=True)
    var = jnp.mean(jnp.square(xf - mu), axis=-1, keepdims=True)
    y = (xf - mu) * lax.rsqrt(var + EPS)
    return (y * g.astype(jnp.float32) + b.astype(jnp.float32)).astype(x.dtype)


def _gated_group_rms_norm(y, z, g):
    bsz, l, d = y.shape
    v = (y * jax.nn.silu(z)).astype(jnp.float32).reshape(bsz, l, SSD_GROUPS, d // SSD_GROUPS)
    v = v * lax.rsqrt(jnp.mean(v * v, axis=-1, keepdims=True) + EPS)
    return (v.reshape(bsz, l, d) * g.astype(jnp.float32)).astype(y.dtype)


def _swiglu(x, w_up, w_down):
    a, b = jnp.split(x @ w_up, 2, axis=-1)
    return (jax.nn.silu(a) * b) @ w_down


def _dwconv_centred(x, w, b):
    k = w.shape[0]
    y = lax.conv_general_dilated(
        x, w.astype(x.dtype)[:, None, :], window_strides=(1,),
        padding=[(k // 2, k // 2)], dimension_numbers=("NWC", "WIO", "NWC"),
        feature_group_count=x.shape[-1])
    return y + b.astype(x.dtype)


def _ssd_scan(x, dt, A, B, C):
    bsz, l, h, p = x.shape
    g, n = B.shape[-2:]
    r = h // g
    q = SSD_CHUNK
    c = l // q
    xc = x.reshape(bsz, c, q, g, r, p)
    dtc = dt.reshape(bsz, c, q, g, r)
    Bc = B.reshape(bsz, c, q, g, n)
    Cc = C.reshape(bsz, c, q, g, n)
    a = dtc * A.astype(jnp.float32).reshape(g, r)
    a_cum = jnp.cumsum(a, axis=2)
    xdt = xc.astype(jnp.float32) * dtc[..., None]
    seg = a_cum[:, :, :, None] - a_cum[:, :, None, :]
    tril = jnp.tril(jnp.ones((q, q), dtype=bool))[:, :, None, None]
    decay = jnp.exp(jnp.where(tril, seg, -jnp.inf))
    scores = jnp.einsum("bcign,bcjgn->bcijg", Cc, Bc).astype(jnp.float32)
    y_diag = jnp.einsum("bcijgr,bcjgrp->bcigrp", scores[..., None] * decay, xdt)
    decay_to_end = jnp.exp(a_cum[:, :, -1:] - a_cum)
    states = jnp.einsum("bcjgn,bcjgrp->bcgrpn", Bc.astype(jnp.float32),
                        xdt * decay_to_end[..., None])
    chunk_decay = jnp.exp(a_cum[:, :, -1])

    def step(s, inp):
        st, dec = inp
        return s * dec[..., None, None] + st, s

    s0 = jnp.zeros((bsz, g, r, p, n), jnp.float32)
    _, prev = lax.scan(step, s0, (jnp.moveaxis(states, 1, 0), jnp.moveaxis(chunk_decay, 1, 0)))
    prev = jnp.moveaxis(prev, 0, 1)
    y_off = jnp.einsum("bcign,bcgrpn->bcigrp", Cc.astype(jnp.float32), prev) * jnp.exp(a_cum)[..., None]
    return (y_diag + y_off).reshape(bsz, l, h, p).astype(x.dtype)


def _conformer_branch(p_glu, dw_w, dw_b, ln_g, ln_b, w_out):
    a, gate = jnp.split(p_glu, 2, axis=-1)
    v = a * jax.nn.sigmoid(gate)
    v = _dwconv_centred(v, dw_w, dw_b)
    v = jax.nn.silu(_layer_norm(v, ln_g, ln_b))
    return v @ w_out


def _ssd_branch(p_z, p_xbc, p_dt, conv_w, conv_b, dt_bias, A_log, D_skip, norm_g, w_out):
    bsz, l, _ = p_z.shape
    xbc = jax.nn.silu(_dwconv_centred(p_xbc, conv_w, conv_b))
    xs, Bm, Cm = jnp.split(xbc, [SSD_INNER, SSD_INNER + SSD_GROUPS * SSD_STATE], axis=-1)
    xs = xs.reshape(bsz, l, SSD_HEADS, SSD_HEAD_DIM)
    Bm = Bm.reshape(bsz, l, SSD_GROUPS, SSD_STATE)
    Cm = Cm.reshape(bsz, l, SSD_GROUPS, SSD_STATE)
    dt_raw = p_dt.astype(jnp.float32).reshape(bsz, l, 2, SSD_HEADS)
    dt = jax.nn.softplus(dt_raw + dt_bias.astype(jnp.float32))
    A = -jnp.exp(A_log.astype(jnp.float32))
    y_fwd = _ssd_scan(xs, dt[:, :, 0], A[0], Bm, Cm)
    flip = lambda t: jnp.flip(t, axis=1)
    y_bwd = flip(_ssd_scan(flip(xs), flip(dt[:, :, 1]), A[1], flip(Bm), flip(Cm)))
    y = y_fwd + y_bwd + xs * D_skip.astype(xs.dtype)[:, None]
    y = _gated_group_rms_norm(y.reshape(bsz, l, SSD_INNER), p_z, norm_g)
    return y @ w_out


def _trunk(x, ffn1_norm, ffn1_w_up, ffn1_w_down, mix_norm, w_in, gate_bias,
           conf_dw_w, conf_dw_b, conf_ln_g, conf_ln_b, conf_w_out,
           ssd_conv_w, ssd_conv_b, ssd_dt_bias, ssd_A_log, ssd_D, ssd_norm, ssd_w_out,
           w_o, ffn2_norm, ffn2_w_up, ffn2_w_down, final_norm):
    splits = [2 * CONF_WIDTH,
              2 * CONF_WIDTH + SSD_INNER,
              2 * CONF_WIDTH + SSD_INNER + SSD_CONV_DIM,
              2 * CONF_WIDTH + SSD_INNER + SSD_CONV_DIM + 2 * SSD_HEADS]
    for i in range(DEPTH):
        x = x + 0.5 * _swiglu(_rms_norm(x, ffn1_norm[i]), ffn1_w_up[i], ffn1_w_down[i])
        u = _rms_norm(x, mix_norm[i])
        proj = u @ w_in[i]
        p_glu, p_z, p_xbc, p_dt, p_gate = jnp.split(proj, splits, axis=-1)
        y_a = _conformer_branch(p_glu, conf_dw_w[i], conf_dw_b[i], conf_ln_g[i], conf_ln_b[i], conf_w_out[i])
        y_b = _ssd_branch(p_z, p_xbc, p_dt, ssd_conv_w[i], ssd_conv_b[i], ssd_dt_bias[i],
                          ssd_A_log[i], ssd_D[i], ssd_norm[i], ssd_w_out[i])
        g_a, g_b = jnp.split(jax.nn.sigmoid(p_gate + gate_bias[i]), 2, axis=-1)
        x = x + (g_a * y_a + g_b * y_b) @ w_o[i]
        x = x + 0.5 * _swiglu(_rms_norm(x, ffn2_norm[i]), ffn2_w_up[i], ffn2_w_down[i])
    return _rms_norm(x, final_norm)


def setup_inputs(seed: int = 0) -> dict:
    key = jax.random.key(seed)
    ks = jax.random.split(key, 32)
    f32 = jnp.float32
    nrm = lambda k, shape, s: jax.random.normal(k, shape, f32) * s
    gain = lambda k, shape: 1.0 + 0.02 * jax.random.normal(k, shape, f32)
    u_dt = jax.random.uniform(ks[20], (DEPTH, 2, SSD_HEADS), f32)
    dt0 = jnp.exp(u_dt * (math.log(0.1) - math.log(0.001)) + math.log(0.001))
    dt_bias = dt0 + jnp.log(-jnp.expm1(-dt0))
    A_log = jnp.log(jax.random.uniform(ks[21], (DEPTH, 2, SSD_HEADS), f32, minval=1.0, maxval=16.0))
    return {
        "x_prompt": jax.random.normal(ks[0], (BATCH, SEQ, D_MODEL), f32),
        "x_sample": jax.random.normal(ks[1], (DEC_BATCH, DEC_SEQ, D_MODEL), f32),
        "ffn1_norm": gain(ks[2], (DEPTH, D_MODEL)),
        "ffn1_w_up": nrm(ks[3], (DEPTH, D_MODEL, 2 * D_FF), D_MODEL ** -0.5),
        "ffn1_w_down": nrm(ks[4], (DEPTH, D_FF, D_MODEL), D_FF ** -0.5),
        "mix_norm": gain(ks[5], (DEPTH, D_MODEL)),
        "w_in": nrm(ks[6], (DEPTH, D_MODEL, N_IN), D_MODEL ** -0.5),
        "gate_bias": nrm(ks[7], (DEPTH, 2 * D_MODEL), 0.02),
        "conf_dw_w": nrm(ks[8], (DEPTH, CONF_KERNEL, CONF_WIDTH), CONF_KERNEL ** -0.5),
        "conf_dw_b": nrm(ks[9], (DEPTH, CONF_WIDTH), 0.02),
        "conf_ln_g": gain(ks[10], (DEPTH, CONF_WIDTH)),
        "conf_ln_b": nrm(ks[11], (DEPTH, CONF_WIDTH), 0.02),
        "conf_w_out": nrm(ks[12], (DEPTH, CONF_WIDTH, D_MODEL), CONF_WIDTH ** -0.5),
        "ssd_conv_w": nrm(ks[13], (DEPTH, SSD_CONV, SSD_CONV_DIM), SSD_CONV ** -0.5),
        "ssd_conv_b": nrm(ks[14], (DEPTH, SSD_CONV_DIM), 0.02),
        "ssd_dt_bias": dt_bias,
        "ssd_A_log": A_log,
        "ssd_D": 1.0 + 0.1 * jax.random.normal(ks[15], (DEPTH, SSD_HEADS), f32),
        "ssd_norm": gain(ks[16], (DEPTH, SSD_INNER)),
        "ssd_w_out": nrm(ks[17], (DEPTH, SSD_INNER, D_MODEL), SSD_INNER ** -0.5),
        "w_o": nrm(ks[18], (DEPTH, D_MODEL, D_MODEL), D_MODEL ** -0.5),
        "ffn2_norm": gain(ks[19], (DEPTH, D_MODEL)),
        "ffn2_w_up": nrm(ks[22], (DEPTH, D_MODEL, 2 * D_FF), D_MODEL ** -0.5),
        "ffn2_w_down": nrm(ks[23], (DEPTH, D_FF, D_MODEL), D_FF ** -0.5),
        "final_norm": gain(ks[24], (D_MODEL,)),
    }


def reference(x_prompt, x_sample, ffn1_norm, ffn1_w_up, ffn1_w_down, mix_norm, w_in, gate_bias,
              conf_dw_w, conf_dw_b, conf_ln_g, conf_ln_b, conf_w_out,
              ssd_conv_w, ssd_conv_b, ssd_dt_bias, ssd_A_log, ssd_D, ssd_norm, ssd_w_out,
              w_o, ffn2_norm, ffn2_w_up, ffn2_w_down, final_norm):
    params = (ffn1_norm, ffn1_w_up, ffn1_w_down, mix_norm, w_in, gate_bias,
              conf_dw_w, conf_dw_b, conf_ln_g, conf_ln_b, conf_w_out,
              ssd_conv_w, ssd_conv_b, ssd_dt_bias, ssd_A_log, ssd_D, ssd_norm, ssd_w_out,
              w_o, ffn2_norm, ffn2_w_up, ffn2_w_down, final_norm)
    y_prompt = _trunk(x_prompt, *params)
    y_sample = _trunk(x_sample, *params)
    return (y_prompt, y_sample)
```

```python
import functools

import jax
import jax.numpy as jnp
from jax import lax
from jax.experimental import pallas as pl
from jax.experimental.pallas import tpu as pltpu

F32 = jnp.float32
BF16 = jnp.bfloat16
EPS = 1e-6

CONF_KERNEL = 31
SSD_CONV = 5
SSD_HEAD_DIM = 64
SSD_GROUPS = 8
SSD_STATE = 128
SSD_CHUNK = 128

V7X_VMEM_BYTES = 64 * 1024 * 1024
V7X_LANES = 128
F32_SUBLANES = 8
BF16_SUBLANES = 16

VMEM_LIMIT = V7X_VMEM_BYTES - 8 * 1024 * 1024

FFN_ROWS = 512
CONF_ROWS = 256
CONF_HALO = 16
CONF_CONV_ROWS = 32
SSDIN_ROWS = 256
SSDIN_HALO = 8
SSDIN_CONV_ROWS = 16
SSD_CHUNKS_PER_STEP = 2


def _rms(x, g):
    return x * lax.rsqrt(jnp.mean(x * x, axis=-1, keepdims=True) + EPS) * g


def _silu(x):
    return x * jax.nn.sigmoid(x)


def _softplus(x):
    return jnp.maximum(x, 0.0) + jnp.log1p(jnp.exp(-jnp.abs(x)))


def _const_spec(shape, layer=None):
    if layer is None:
        return pl.BlockSpec(shape, lambda *_: (0,) * len(shape), pipeline_mode=pl.Buffered(1))
    return pl.BlockSpec((None,) + shape, lambda *_: (layer,) + (0,) * len(shape),
                        pipeline_mode=pl.Buffered(1))


def _layer_block(shape, layer, col_block=0):
    return pl.BlockSpec((None,) + shape, lambda *_: (layer, 0, col_block),
                        pipeline_mode=pl.Buffered(1))


def _split3(x):
    hi = x.astype(BF16)
    r1 = x - hi.astype(F32)
    mid = r1.astype(BF16)
    lo = (r1 - mid.astype(F32)).astype(BF16)
    return hi, mid, lo


def _split2(x):
    hi = x.astype(BF16)
    lo = (x - hi.astype(F32)).astype(BF16)
    return hi, lo


def _ffn_kernel(x_ref, g_ref, wup_ref, wdn_ref, fg_ref, o_ref, *, d_ff, final_norm):
    x = x_ref[...]
    xn = _rms(x, g_ref[...]).astype(BF16)
    h = jnp.dot(xn, wup_ref[...], preferred_element_type=F32)
    act = (_silu(h[:, :d_ff]) * h[:, d_ff:]).astype(BF16)
    y = jnp.dot(act, wdn_ref[...], preferred_element_type=F32)
    out = x + 0.5 * y
    if final_norm:
        out = _rms(out, fg_ref[...])
    o_ref[...] = out


def _ffn(x2, norm, w_up, w_down, final_g, layer, final_norm):
    t, d = x2.shape
    d_ff = w_down.shape[1]
    assert t % FFN_ROWS == 0
    return pl.pallas_call(
        functools.partial(_ffn_kernel, d_ff=d_ff, final_norm=final_norm),
        grid=(t // FFN_ROWS,),
        in_specs=[
            pl.BlockSpec((FFN_ROWS, d), lambda i: (i, 0)),
            _const_spec((1, d), layer),
            _const_spec((d, 2 * d_ff), layer),
            _const_spec((d_ff, d), layer),
            _const_spec((1, d)),
        ],
        out_specs=pl.BlockSpec((FFN_ROWS, d), lambda i: (i, 0)),
        out_shape=jax.ShapeDtypeStruct((t, d), F32),
        compiler_params=pltpu.CompilerParams(
            dimension_semantics=("parallel",), vmem_limit_bytes=VMEM_LIMIT),
        name="ffn",
    )(x2, norm, w_up, w_down, final_g)


def _conf_kernel(xm_ref, xp_ref, xn_ref, g_ref, wglu_ref, wga_ref, bga_ref, dww_ref, dwb_ref,
                 lng_ref, lnb_ref, wco_ref, o_ref, vs_ref, hs_ref, *, rows, width):
    i = pl.program_id(1)
    last = pl.num_programs(1) - 1
    g = g_ref[...]
    ext = rows + 2 * CONF_HALO

    xe = jnp.concatenate([xp_ref[...], xm_ref[...], xn_ref[...]], axis=0)
    u = _rms(xe, g).astype(BF16)
    p = jnp.dot(u, wglu_ref[...], preferred_element_type=F32)
    v = p[:, :width] * jax.nn.sigmoid(p[:, width:])
    r = lax.broadcasted_iota(jnp.int32, (ext, 1), 0)
    valid = jnp.logical_and(jnp.logical_or(r >= CONF_HALO, i > 0),
                            jnp.logical_or(r < CONF_HALO + rows, i < last))
    vs_ref[...] = jnp.where(valid, v, 0.0)

    first_tap = CONF_HALO - CONF_KERNEL // 2

    for r0 in range(0, rows, CONF_CONV_ROWS):
        acc = jnp.broadcast_to(dwb_ref[...], (CONF_CONV_ROWS, width))
        for k in range(CONF_KERNEL):
            acc = acc + dww_ref[k:k + 1, :] * vs_ref[r0 + first_tap + k:r0 + first_tap + k + CONF_CONV_ROWS, :]
        mu = jnp.mean(acc, axis=-1, keepdims=True)
        dlt = acc - mu
        var = jnp.mean(dlt * dlt, axis=-1, keepdims=True)
        y = dlt * lax.rsqrt(var + EPS) * lng_ref[...] + lnb_ref[...]
        hs_ref[r0:r0 + CONF_CONV_ROWS, :] = _silu(y).astype(BF16)

    ya = jnp.dot(hs_ref[...], wco_ref[...], preferred_element_type=F32)
    um = _rms(xm_ref[...], g).astype(BF16)
    ga = jax.nn.sigmoid(jnp.dot(um, wga_ref[...], preferred_element_type=F32) + bga_ref[...])
    o_ref[...] = (ga * ya).astype(BF16)


def _conf(x3, p, layer):
    b, l, d = x3.shape
    rows = CONF_ROWS
    width = p["conf_dw_w"].shape[-1]
    assert l % rows == 0 and rows % CONF_HALO == 0
    hb = rows // CONF_HALO
    nhb = l // CONF_HALO
    return pl.pallas_call(
        functools.partial(_conf_kernel, rows=rows, width=width),
        grid=(b, l // rows),
        in_specs=[
            pl.BlockSpec((None, rows, d), lambda bi, i: (bi, i, 0)),
            pl.BlockSpec((None, CONF_HALO, d), lambda bi, i: (bi, jnp.maximum(i * hb - 1, 0), 0)),
            pl.BlockSpec((None, CONF_HALO, d), lambda bi, i: (bi, jnp.minimum((i + 1) * hb, nhb - 1), 0)),
            _const_spec((1, d), layer),
            _layer_block((d, 2 * width), layer, 0),
            _layer_block((d, d), layer, 0),
            _layer_block((1, d), layer, 0),
            _const_spec((CONF_KERNEL, width), layer),
            _const_spec((1, width), layer),
            _const_spec((1, width), layer),
            _const_spec((1, width), layer),
            _const_spec((width, d), layer),
        ],
        out_specs=pl.BlockSpec((None, rows, d), lambda bi, i: (bi, i, 0)),
        out_shape=jax.ShapeDtypeStruct((b, l, d), BF16),
        scratch_shapes=[pltpu.VMEM((rows + 2 * CONF_HALO, width), F32),
                        pltpu.VMEM((rows, width), BF16)],
        compiler_params=pltpu.CompilerParams(
            dimension_semantics=("parallel", "parallel"), vmem_limit_bytes=VMEM_LIMIT),
        name="conf",
    )(x3, x3, x3, p["mix_norm"], p["w_main"], p["w_gate"], p["gate_bias"], p["conf_dw_w"],
      p["conf_dw_b"], p["conf_ln_g"], p["conf_ln_b"], p["conf_w_out"])


def _ssdin_kernel(xm_ref, xp_ref, xn_ref, g_ref, wz_ref, wxbc_ref, wdt_ref, dtb_ref, wgb_ref, bgb_ref,
                  cw_ref, cb_ref, zs_ref, xbc_ref, dtx_ref, gb_ref, xs_ref, *, rows):
    i = pl.program_id(1)
    last = pl.num_programs(1) - 1
    g = g_ref[...]
    ext = rows + 2 * SSDIN_HALO
    cdim = xs_ref.shape[-1]

    xe = jnp.concatenate([xp_ref[...], xm_ref[...], xn_ref[...]], axis=0)
    u = _rms(xe, g).astype(BF16)
    xbc = jnp.dot(u, wxbc_ref[...], preferred_element_type=F32)
    r = lax.broadcasted_iota(jnp.int32, (ext, 1), 0)
    valid = jnp.logical_and(jnp.logical_or(r >= SSDIN_HALO, i > 0),
                            jnp.logical_or(r < SSDIN_HALO + rows, i < last))
    xs_ref[...] = jnp.where(valid, xbc, 0.0)

    first_tap = SSDIN_HALO - SSD_CONV // 2

    for r0 in range(0, rows, SSDIN_CONV_ROWS):
        acc = jnp.broadcast_to(cb_ref[...], (SSDIN_CONV_ROWS, cdim))
        for k in range(SSD_CONV):
            acc = acc + cw_ref[k:k + 1, :] * xs_ref[r0 + first_tap + k:r0 + first_tap + k + SSDIN_CONV_ROWS, :]
        xbc_ref[r0:r0 + SSDIN_CONV_ROWS, :] = _silu(acc).astype(BF16)

    um = _rms(xm_ref[...], g).astype(BF16)
    zs_ref[...] = _silu(jnp.dot(um, wz_ref[...], preferred_element_type=F32)).astype(BF16)
    dtx_ref[...] = _softplus(jnp.dot(um, wdt_ref[...], preferred_element_type=F32) + dtb_ref[...])
    gb_ref[...] = jax.nn.sigmoid(
        jnp.dot(um, wgb_ref[...], preferred_element_type=F32) + bgb_ref[...]).astype(BF16)


def _ssdin(x3, p, layer):
    b, l, d = x3.shape
    rows = SSDIN_ROWS
    inner = p["ssd_norm"].shape[-1]
    cdim = p["ssd_conv_w"].shape[-1]
    assert l % rows == 0 and rows % SSDIN_HALO == 0
    hb = rows // SSDIN_HALO
    nhb = l // SSDIN_HALO
    tok = lambda w: pl.BlockSpec((None, rows, w), lambda bi, i: (bi, i, 0))
    return pl.pallas_call(
        functools.partial(_ssdin_kernel, rows=rows),
        grid=(b, l // rows),
        in_specs=[
            tok(d),
            pl.BlockSpec((None, SSDIN_HALO, d), lambda bi, i: (bi, jnp.maximum(i * hb - 1, 0), 0)),
            pl.BlockSpec((None, SSDIN_HALO, d), lambda bi, i: (bi, jnp.minimum((i + 1) * hb, nhb - 1), 0)),
            _const_spec((1, d), layer),
            _layer_block((d, inner), layer, 1),
            _layer_block((d, cdim), layer, 1),
            _const_spec((d, V7X_LANES), layer),
            _const_spec((1, V7X_LANES), layer),
            _layer_block((d, d), layer, 1),
            _layer_block((1, d), layer, 1),
            _const_spec((SSD_CONV, cdim), layer),
            _const_spec((1, cdim), layer),
        ],
        out_specs=[tok(inner), tok(cdim), tok(V7X_LANES), tok(d)],
        out_shape=[jax.ShapeDtypeStruct((b, l, inner), BF16),
                   jax.ShapeDtypeStruct((b, l, cdim), BF16),
                   jax.ShapeDtypeStruct((b, l, V7X_LANES), F32),
                   jax.ShapeDtypeStruct((b, l, d), BF16)],
        scratch_shapes=[pltpu.VMEM((rows + 2 * SSDIN_HALO, cdim), F32)],
        compiler_params=pltpu.CompilerParams(
            dimension_semantics=("parallel", "parallel"), vmem_limit_bytes=VMEM_LIMIT),
        name="ssd_in",
    )(x3, x3, x3, p["mix_norm"], p["w_main"], p["w_main"], p["w_dt"], p["dt_bias"], p["w_gate"],
      p["gate_bias"], p["ssd_conv_w"], p["ssd_conv_b"])


def _chunk_stats(dtx, avec, ltri):
    a = dtx * avec
    hi, mid, lo = _split3(a)
    c = jnp.dot(ltri, jnp.concatenate([hi, mid, lo], axis=1), preferred_element_type=F32)
    n = a.shape[1]
    cum = c[:, :n] + c[:, n:2 * n] + c[:, 2 * n:]
    return a, cum, cum[SSD_CHUNK - 1:SSD_CHUNK, :]


def _expand(v, e_ref):
    hi, lo = _split2(v)
    return jnp.dot(jnp.concatenate([hi, lo], axis=1), e_ref[...], preferred_element_type=F32)


def _ssd_fwd_kernel(xs_ref, b_ref, dtx_ref, alog_ref, ltri_ref, e_ref, pf_ref, st_ref, *, heads):
    @pl.when(pl.program_id(1) == 0)
    def _():
        st_ref[...] = jnp.zeros_like(st_ref)

    lane = lax.broadcasted_iota(jnp.int32, (1, V7X_LANES), 1)
    avec = -jnp.exp(alog_ref[...])
    gw = st_ref.shape[1] // SSD_GROUPS
    in_fwd_copy = jnp.logical_and(lane >= 2 * heads, lane < 3 * heads)
    for k in range(SSD_CHUNKS_PER_STEP):
        rs = slice(k * SSD_CHUNK, (k + 1) * SSD_CHUNK)
        pf_ref[k] = st_ref[...].astype(BF16)
        dtx = dtx_ref[rs, :]
        _, cum, tot = _chunk_stats(dtx, avec, ltri_ref[...])
        w = jnp.where(in_fwd_copy, dtx * jnp.exp(tot - cum), 0.0)
        dec = jnp.where(in_fwd_copy, jnp.exp(tot), 0.0)
        wx = _expand(jnp.concatenate([w, jnp.broadcast_to(dec, (BF16_SUBLANES, V7X_LANES))], axis=0), e_ref)
        xw = (xs_ref[rs, :].astype(F32) * wx[:SSD_CHUNK]).astype(BF16)
        decx = wx[SSD_CHUNK:SSD_CHUNK + 1]
        for g in range(SSD_GROUPS):
            cs = slice(g * gw, (g + 1) * gw)
            bg = b_ref[rs, g * SSD_STATE:(g + 1) * SSD_STATE]
            s_t = lax.dot_general(bg, xw[:, cs], (((0,), (0,)), ((), ())), preferred_element_type=F32)
            st_ref[:, cs] = st_ref[:, cs] * decx[:, cs] + s_t


def _ssd_fwd(xbc, dtx, p, layer, consts):
    b, l, cdim = xbc.shape
    inner = p["ssd_norm"].shape[-1]
    heads = inner // SSD_HEAD_DIM
    rows = SSD_CHUNKS_PER_STEP * SSD_CHUNK
    nc = l // SSD_CHUNK
    assert l % rows == 0 and 4 * heads == V7X_LANES
    bc = SSD_GROUPS * SSD_STATE
    return pl.pallas_call(
        functools.partial(_ssd_fwd_kernel, heads=heads),
        grid=(b, l // rows),
        in_specs=[
            pl.BlockSpec((None, rows, inner), lambda bi, j: (bi, j, 0)),
            pl.BlockSpec((None, rows, bc), lambda bi, j: (bi, j, inner // bc)),
            pl.BlockSpec((None, rows, V7X_LANES), lambda bi, j: (bi, j, 0)),
            _const_spec((1, V7X_LANES), layer),
            _const_spec((SSD_CHUNK, SSD_CHUNK)),
            _const_spec((2 * V7X_LANES, inner)),
        ],
        out_specs=pl.BlockSpec((None, SSD_CHUNKS_PER_STEP, SSD_STATE, inner), lambda bi, j: (bi, j, 0, 0)),
        out_shape=jax.ShapeDtypeStruct((b, nc, SSD_STATE, inner), BF16),
        scratch_shapes=[pltpu.VMEM((SSD_STATE, inner), F32)],
        compiler_params=pltpu.CompilerParams(
            dimension_semantics=("parallel", "arbitrary"), vmem_limit_bytes=VMEM_LIMIT),
        name="ssd_fwd",
    )(xbc, xbc, dtx, p["a_log"], consts["ltri"], consts["e_fwd"])


def _ssd_out_kernel(xs_ref, b_ref, c_ref, zs_ref, dtx_ref, pf_ref, yag_ref, gb_ref, x_ref,
                    alog_ref, dx_ref, ng_ref, wso_ref, wo_ref, ltri_ref, ui_ref, e_ref,
                    o_ref, st_ref, vn_ref, *, heads):
    @pl.when(pl.program_id(1) == 0)
    def _():
        st_ref[...] = jnp.zeros_like(st_ref)

    inner = st_ref.shape[1]
    gw = inner // SSD_GROUPS
    hpg = heads // SSD_GROUPS
    lane = lax.broadcasted_iota(jnp.int32, (1, V7X_LANES), 1)
    avec = -jnp.exp(alog_ref[...])
    ri = lax.broadcasted_iota(jnp.int32, (SSD_CHUNK, SSD_CHUNK), 0)
    ci = lax.broadcasted_iota(jnp.int32, (SSD_CHUNK, SSD_CHUNK), 1)
    tril = ci <= ri
    triu = ci >= ri
    br = lax.broadcasted_iota(jnp.int32, (hpg * SSD_CHUNK, gw), 0) // SSD_CHUNK
    bcl = lax.broadcasted_iota(jnp.int32, (hpg * SSD_CHUNK, gw), 1) // SSD_HEAD_DIM
    blockdiag = br == bcl
    neg_inf = -jnp.inf

    for k in reversed(range(SSD_CHUNKS_PER_STEP)):
        rs = slice(k * SSD_CHUNK, (k + 1) * SSD_CHUNK)
        dtx = dtx_ref[rs, :]
        a, cum, tot = _chunk_stats(dtx, avec, ltri_ref[...])
        ecum = cum - a
        yh, ym, yl = _split3(jnp.where(lane < 2 * heads, a, dtx))
        t3 = lax.dot_general(jnp.concatenate([yh, ym, yl], axis=1), ui_ref[...],
                             (((0,), (0,)), ((), ())), preferred_element_type=F32)
        tt = t3[:V7X_LANES] + t3[V7X_LANES:2 * V7X_LANES] + t3[2 * V7X_LANES:]
        cum_f_t = tt[0:heads, :SSD_CHUNK]
        ecum_b_t = tt[heads:2 * heads, :SSD_CHUNK] - tt[heads:2 * heads, SSD_CHUNK:]
        dt_f_t = tt[2 * heads:3 * heads, SSD_CHUNK:]
        dt_b_t = tt[3 * heads:4 * heads, SSD_CHUNK:]

        scal = jnp.where(lane < heads, jnp.exp(cum),
                         jnp.where(lane < 2 * heads, jnp.exp(tot - ecum),
                                   jnp.where(lane >= 3 * heads, dtx * jnp.exp(ecum), 0.0)))
        dec = jnp.where(lane >= 3 * heads, jnp.exp(tot), 0.0)
        ex = _expand(jnp.concatenate([scal, jnp.broadcast_to(dec, (BF16_SUBLANES, V7X_LANES))], axis=0), e_ref)
        e_f = ex[:SSD_CHUNK, 0:inner]
        e_b = ex[:SSD_CHUNK, inner:2 * inner]
        w_b = ex[:SSD_CHUNK, 2 * inner:]
        dec_b = ex[SSD_CHUNK:SSD_CHUNK + 1, 2 * inner:]

        xs = xs_ref[rs, :]
        xw = (xs.astype(F32) * w_b).astype(BF16)
        for g in range(SSD_GROUPS):
            cs = slice(g * gw, (g + 1) * gw)
            ns = slice(g * SSD_STATE, (g + 1) * SSD_STATE)
            bg = b_ref[rs, ns]
            cg = c_ref[rs, ns]
            s = lax.dot_general(cg, bg, (((1,), (1,)), ((), ())), preferred_element_type=F32)
            sg = []
            for hh in range(hpg):
                h = g * hpg + hh
                lf = jnp.exp(jnp.where(tril, cum[:, h:h + 1] - cum_f_t[h:h + 1, :], neg_inf))
                lb = jnp.exp(jnp.where(triu, ecum_b_t[h:h + 1, :] - ecum[:, heads + h:heads + h + 1], neg_inf))
                gmat = lf * dt_f_t[h:h + 1, :] + lb * dt_b_t[h:h + 1, :]
                sg.append((s * gmat).astype(BF16))
            xg = xs[:, cs]
            xbd = jnp.where(blockdiag, jnp.concatenate([xg] * hpg, axis=0), jnp.zeros((), BF16))
            y = jnp.dot(jnp.concatenate(sg, axis=1), xbd, preferred_element_type=F32)
            y = y + e_f[:, cs] * jnp.dot(cg, pf_ref[k, :, cs], preferred_element_type=F32)
            y = y + e_b[:, cs] * jnp.dot(cg, st_ref[:, cs].astype(BF16), preferred_element_type=F32)
            y = y + xg.astype(F32) * dx_ref[:, cs]
            s_t = lax.dot_general(bg, xw[:, cs], (((0,), (0,)), ((), ())), preferred_element_type=F32)
            st_ref[:, cs] = st_ref[:, cs] * dec_b[:, cs] + s_t
            v = y * zs_ref[rs, cs].astype(F32)
            vn = v * lax.rsqrt(jnp.mean(v * v, axis=-1, keepdims=True) + EPS) * ng_ref[:, cs]
            vn_ref[rs, cs] = vn.astype(BF16)

    yb = jnp.dot(vn_ref[...], wso_ref[...], preferred_element_type=F32)
    mix = (yag_ref[...].astype(F32) + gb_ref[...].astype(F32) * yb).astype(BF16)
    o_ref[...] = x_ref[...] + jnp.dot(mix, wo_ref[...], preferred_element_type=F32)


def _ssd_out(xbc, zs, dtx, pf, yag, gb, x3, p, layer, consts):
    b, l, d = x3.shape
    inner = p["ssd_norm"].shape[-1]
    heads = inner // SSD_HEAD_DIM
    rows = SSD_CHUNKS_PER_STEP * SSD_CHUNK
    nblk = l // rows
    bc = SSD_GROUPS * SSD_STATE
    rev = lambda bi, j: (bi, nblk - 1 - j, 0)
    return pl.pallas_call(
        functools.partial(_ssd_out_kernel, heads=heads),
        grid=(b, nblk),
        in_specs=[
            pl.BlockSpec((None, rows, inner), rev),
            pl.BlockSpec((None, rows, bc), lambda bi, j: (bi, nblk - 1 - j, inner // bc)),
            pl.BlockSpec((None, rows, bc), lambda bi, j: (bi, nblk - 1 - j, inner // bc + 1)),
            pl.BlockSpec((None, rows, inner), rev),
            pl.BlockSpec((None, rows, V7X_LANES), rev),
            pl.BlockSpec((None, SSD_CHUNKS_PER_STEP, SSD_STATE, inner), lambda bi, j: (bi, nblk - 1 - j, 0, 0)),
            pl.BlockSpec((None, rows, d), rev),
            pl.BlockSpec((None, rows, d), rev),
            pl.BlockSpec((None, rows, d), rev),
            _const_spec((1, V7X_LANES), layer),
            _const_spec((1, inner), layer),
            _const_spec((1, inner), layer),
            _const_spec((inner, d), layer),
            _const_spec((d, d), layer),
            _const_spec((SSD_CHUNK, SSD_CHUNK)),
            _const_spec((SSD_CHUNK, 2 * SSD_CHUNK)),
            _const_spec((2 * V7X_LANES, 3 * inner)),
        ],
        out_specs=pl.BlockSpec((None, rows, d), rev),
        out_shape=jax.ShapeDtypeStruct((b, l, d), F32),
        scratch_shapes=[pltpu.VMEM((SSD_STATE, inner), F32), pltpu.VMEM((rows, inner), BF16)],
        compiler_params=pltpu.CompilerParams(
            dimension_semantics=("parallel", "arbitrary"), vmem_limit_bytes=VMEM_LIMIT),
        name="ssd_out",
    )(xbc, xbc, xbc, zs, dtx, pf, yag, gb, x3, p["a_log"], p["d_skip"], p["ssd_norm"],
      p["ssd_w_out"], p["w_o"], consts["ltri"], consts["ui"], consts["e_out"])


def _scan_constants(inner):
    heads = inner // SSD_HEAD_DIM
    t = jnp.arange(SSD_CHUNK)
    ltri = (t[None, :] <= t[:, None]).astype(BF16)
    ui = jnp.concatenate([(t[:, None] <= t[None, :]).astype(BF16),
                          jnp.eye(SSD_CHUNK, dtype=BF16)], axis=1)
    lane = jnp.arange(V7X_LANES)
    col_head = jnp.arange(inner) // SSD_HEAD_DIM

    def expander(lane_groups):
        blocks = [(lane[:, None] == g * heads + col_head[None, :]).astype(BF16) for g in lane_groups]
        e = jnp.concatenate(blocks, axis=1)
        return jnp.concatenate([e, e], axis=0)

    return {"ltri": ltri, "ui": ui, "e_fwd": expander([2]), "e_out": expander([0, 1, 3])}


def kernel(x_prompt, x_sample, ffn1_norm, ffn1_w_up, ffn1_w_down, mix_norm, w_in, gate_bias, conf_dw_w, conf_dw_b, conf_ln_g, conf_ln_b, conf_w_out, ssd_conv_w, ssd_conv_b, ssd_dt_bias, ssd_A_log, ssd_D, ssd_norm, ssd_w_out, w_o, ffn2_norm, ffn2_w_up, ffn2_w_down, final_norm):
    depth, d = ffn1_norm.shape
    width = conf_dw_w.shape[-1]
    inner = ssd_norm.shape[-1]
    cdim = ssd_conv_w.shape[-1]
    heads = inner // SSD_HEAD_DIM
    n_main = 2 * width + inner + cdim
    assert x_prompt.shape[1] == x_sample.shape[1]

    row = lambda v: v.reshape(depth, 1, -1).astype(F32)
    w_dt = w_in[:, :, n_main:n_main + 2 * heads]
    p = {
        "mix_norm": row(mix_norm),
        "w_main": w_in[:, :, :n_main].astype(BF16),
        "w_gate": w_in[:, :, n_main + 2 * heads:].astype(BF16),
        "w_dt": jnp.concatenate([w_dt, w_dt], axis=-1).astype(BF16),
        "dt_bias": jnp.tile(ssd_dt_bias.reshape(depth, 1, 2 * heads), (1, 1, 2)).astype(F32),
        "gate_bias": row(gate_bias),
        "conf_dw_w": conf_dw_w.astype(F32), "conf_dw_b": row(conf_dw_b),
        "conf_ln_g": row(conf_ln_g), "conf_ln_b": row(conf_ln_b),
        "conf_w_out": conf_w_out.astype(BF16),
        "ssd_conv_w": ssd_conv_w.astype(F32), "ssd_conv_b": row(ssd_conv_b),
        "a_log": jnp.tile(ssd_A_log.reshape(depth, 1, 2 * heads), (1, 1, 2)).astype(F32),
        "d_skip": jnp.repeat(ssd_D, SSD_HEAD_DIM, axis=-1).reshape(depth, 1, inner).astype(F32),
        "ssd_norm": row(ssd_norm),
        "ssd_w_out": ssd_w_out.astype(BF16),
        "w_o": w_o.astype(BF16),
    }
    f1n, f2n = row(ffn1_norm), row(ffn2_norm)
    f1u, f1d = ffn1_w_up.astype(BF16), ffn1_w_down.astype(BF16)
    f2u, f2d = ffn2_w_up.astype(BF16), ffn2_w_down.astype(BF16)
    fg = final_norm.reshape(1, d).astype(F32)
    consts = _scan_constants(inner)

    nb_prompt = x_prompt.shape[0]
    x3 = jnp.concatenate([x_prompt, x_sample], axis=0)
    b, l, _ = x3.shape
    for i in range(depth):
        x3 = _ffn(x3.reshape(b * l, d), f1n, f1u, f1d, fg, i, False).reshape(b, l, d)
        yag = _conf(x3, p, i)
        zs, xbc, dtx, gb = _ssdin(x3, p, i)
        pf = _ssd_fwd(xbc, dtx, p, i, consts)
        x3 = _ssd_out(xbc, zs, dtx, pf, yag, gb, x3, p, i, consts)
        x3 = _ffn(x3.reshape(b * l, d), f2n, f2u, f2d, fg, i, i == depth - 1).reshape(b, l, d)
    return (x3[:nb_prompt], x3[nb_prompt:])
```

```python
import functools

import jax
import jax.numpy as jnp
from jax import lax
from jax.experimental import pallas as pl
from jax.experimental.pallas import tpu as pltpu

F32 = jnp.float32
BF16 = jnp.bfloat16
EPS = 1e-6

CONF_KERNEL = 31
SSD_CONV = 5
SSD_HEAD_DIM = 64
SSD_GROUPS = 8
SSD_STATE = 128
SSD_CHUNK = 128

V7X_VMEM_BYTES = 64 * 1024 * 1024
V7X_LANES = 128
F32_SUBLANES = 8
BF16_SUBLANES = 16

VMEM_LIMIT = V7X_VMEM_BYTES - 8 * 1024 * 1024

FFN_ROWS = 512
CONF_ROWS = 256
CONF_HALO = 16
CONF_CONV_ROWS = 128
CONF_COLS = 256
CONF_LN_ROWS = 32
CONF_OUT_ROWS = 128
SSDIN_ROWS = 256
SSDIN_HALO = 8
SSDIN_CONV_ROWS = 128
SSDIN_COLS = 256
SSD_CHUNKS_PER_STEP = 2


def _rms(x, g):
    return x * lax.rsqrt(jnp.mean(x * x, axis=-1, keepdims=True) + EPS) * g


def _silu(x):
    return x * jax.nn.sigmoid(x)


def _softplus(x):
    return jnp.maximum(x, 0.0) + jnp.log1p(jnp.exp(-jnp.abs(x)))


def _const_spec(shape, layer=None):
    if layer is None:
        return pl.BlockSpec(shape, lambda *_: (0,) * len(shape), pipeline_mode=pl.Buffered(1))
    return pl.BlockSpec((None,) + shape, lambda *_: (layer,) + (0,) * len(shape),
                        pipeline_mode=pl.Buffered(1))


def _layer_block(shape, layer, col_block=0):
    return pl.BlockSpec((None,) + shape, lambda *_: (layer, 0, col_block),
                        pipeline_mode=pl.Buffered(1))


def _split3(x):
    hi = x.astype(BF16)
    r1 = x - hi.astype(F32)
    mid = r1.astype(BF16)
    lo = (r1 - mid.astype(F32)).astype(BF16)
    return hi, mid, lo


def _split2(x):
    hi = x.astype(BF16)
    lo = (x - hi.astype(F32)).astype(BF16)
    return hi, lo


def _ffn_kernel(x_ref, g_ref, wup_ref, wdn_ref, fg_ref, o_ref, *, d_ff, final_norm):
    x = x_ref[...]
    xn = _rms(x, g_ref[...]).astype(BF16)
    h = jnp.dot(xn, wup_ref[...], preferred_element_type=F32)
    act = (_silu(h[:, :d_ff]) * h[:, d_ff:]).astype(BF16)
    y = jnp.dot(act, wdn_ref[...], preferred_element_type=F32)
    out = x + 0.5 * y
    if final_norm:
        out = _rms(out, fg_ref[...])
    o_ref[...] = out


def _ffn(x2, norm, w_up, w_down, final_g, layer, final_norm):
    t, d = x2.shape
    d_ff = w_down.shape[1]
    assert t % FFN_ROWS == 0
    return pl.pallas_call(
        functools.partial(_ffn_kernel, d_ff=d_ff, final_norm=final_norm),
        grid=(t // FFN_ROWS,),
        in_specs=[
            pl.BlockSpec((FFN_ROWS, d), lambda i: (i, 0)),
            _const_spec((1, d), layer),
            _const_spec((d, 2 * d_ff), layer),
            _const_spec((d_ff, d), layer),
            _const_spec((1, d)),
        ],
        out_specs=pl.BlockSpec((FFN_ROWS, d), lambda i: (i, 0)),
        out_shape=jax.ShapeDtypeStruct((t, d), F32),
        compiler_params=pltpu.CompilerParams(
            dimension_semantics=("parallel",), vmem_limit_bytes=VMEM_LIMIT),
        name="ffn",
    )(x2, norm, w_up, w_down, final_g)


def _conf_kernel(xm_ref, xp_ref, xn_ref, g_ref, wglu_ref, wga_ref, bga_ref, dww_ref, dwb_ref,
                 lng_ref, lnb_ref, wco_ref, o_ref, vs_ref, cv_ref, hs_ref, *, rows, width):
    i = pl.program_id(1)
    last = pl.num_programs(1) - 1
    g = g_ref[...]
    ext = rows + 2 * CONF_HALO

    xe = jnp.concatenate([xp_ref[...], xm_ref[...], xn_ref[...]], axis=0)
    u = _rms(xe, g).astype(BF16)
    r = lax.broadcasted_iota(jnp.int32, (ext, 1), 0)
    valid = jnp.logical_and(jnp.logical_or(r >= CONF_HALO, i > 0),
                            jnp.logical_or(r < CONF_HALO + rows, i < last))
    first_tap = CONF_HALO - CONF_KERNEL // 2

    def glu(c0):
        pa = jnp.dot(u, wglu_ref[:, c0:c0 + CONF_COLS], preferred_element_type=F32)
        pg = jnp.dot(u, wglu_ref[:, width + c0:width + c0 + CONF_COLS], preferred_element_type=F32)
        v = jnp.where(valid, pa * jax.nn.sigmoid(pg), 0.0)
        for j in range(CONF_COLS // V7X_LANES):
            vs_ref[c0 // V7X_LANES + j] = v[:, j * V7X_LANES:(j + 1) * V7X_LANES]

    def taps(c0):
        for j in range(c0 // V7X_LANES, (c0 + CONF_COLS) // V7X_LANES):
            cs = slice(j * V7X_LANES, (j + 1) * V7X_LANES)
            for r0 in range(0, rows, CONF_CONV_ROWS):
                acc = jnp.broadcast_to(dwb_ref[:, cs], (CONF_CONV_ROWS, V7X_LANES))
                for k in range(CONF_KERNEL):
                    a0 = r0 + first_tap + k
                    acc = acc + dww_ref[k:k + 1, cs] * vs_ref[j, a0:a0 + CONF_CONV_ROWS, :]
                cv_ref[r0:r0 + CONF_CONV_ROWS, cs] = acc

    um = _rms(xm_ref[...], g).astype(BF16)
    glu(0)
    for c0 in range(0, width, CONF_COLS):
        if c0 + CONF_COLS < width:
            glu(c0 + CONF_COLS)
        else:
            ga = jax.nn.sigmoid(jnp.dot(um, wga_ref[...], preferred_element_type=F32) + bga_ref[...])
        taps(c0)

    for r0 in range(0, rows, CONF_OUT_ROWS):
        for r1 in range(r0, r0 + CONF_OUT_ROWS, CONF_LN_ROWS):
            acc = cv_ref[r1:r1 + CONF_LN_ROWS, :]
            mu = jnp.mean(acc, axis=-1, keepdims=True)
            dlt = acc - mu
            var = jnp.mean(dlt * dlt, axis=-1, keepdims=True)
            y = dlt * lax.rsqrt(var + EPS) * lng_ref[...] + lnb_ref[...]
            hs_ref[r1:r1 + CONF_LN_ROWS, :] = _silu(y).astype(BF16)
        ya = jnp.dot(hs_ref[r0:r0 + CONF_OUT_ROWS, :], wco_ref[...], preferred_element_type=F32)
        o_ref[r0:r0 + CONF_OUT_ROWS, :] = (ga[r0:r0 + CONF_OUT_ROWS] * ya).astype(BF16)


def _conf(x3, p, layer):
    b, l, d = x3.shape
    rows = CONF_ROWS
    width = p["conf_dw_w"].shape[-1]
    assert l % rows == 0 and rows % CONF_HALO == 0
    hb = rows // CONF_HALO
    nhb = l // CONF_HALO
    return pl.pallas_call(
        functools.partial(_conf_kernel, rows=rows, width=width),
        grid=(b, l // rows),
        in_specs=[
            pl.BlockSpec((None, rows, d), lambda bi, i: (bi, i, 0)),
            pl.BlockSpec((None, CONF_HALO, d), lambda bi, i: (bi, jnp.maximum(i * hb - 1, 0), 0)),
            pl.BlockSpec((None, CONF_HALO, d), lambda bi, i: (bi, jnp.minimum((i + 1) * hb, nhb - 1), 0)),
            _const_spec((1, d), layer),
            _layer_block((d, 2 * width), layer, 0),
            _layer_block((d, d), layer, 0),
            _layer_block((1, d), layer, 0),
            _const_spec((CONF_KERNEL, width), layer),
            _const_spec((1, width), layer),
            _const_spec((1, width), layer),
            _const_spec((1, width), layer),
            _const_spec((width, d), layer),
        ],
        out_specs=pl.BlockSpec((None, rows, d), lambda bi, i: (bi, i, 0)),
        out_shape=jax.ShapeDtypeStruct((b, l, d), BF16),
        scratch_shapes=[pltpu.VMEM((width // V7X_LANES, rows + 2 * CONF_HALO, V7X_LANES), F32),
                        pltpu.VMEM((rows, width), F32),
                        pltpu.VMEM((rows, width), BF16)],
        compiler_params=pltpu.CompilerParams(
            dimension_semantics=("parallel", "parallel"), vmem_limit_bytes=VMEM_LIMIT),
        name="conf",
    )(x3, x3, x3, p["mix_norm"], p["w_main"], p["w_gate"], p["gate_bias"], p["conf_dw_w"],
      p["conf_dw_b"], p["conf_ln_g"], p["conf_ln_b"], p["conf_w_out"])


def _ssdin_kernel(xm_ref, xp_ref, xn_ref, g_ref, wz_ref, wxbc_ref, wdt_ref, dtb_ref, wgb_ref, bgb_ref,
                  cw_ref, cb_ref, zs_ref, xbc_ref, dtx_ref, gb_ref, xs_ref, *, rows):
    i = pl.program_id(1)
    last = pl.num_programs(1) - 1
    g = g_ref[...]
    ext = rows + 2 * SSDIN_HALO
    cdim = xbc_ref.shape[-1]

    xe = jnp.concatenate([xp_ref[...], xm_ref[...], xn_ref[...]], axis=0)
    u = _rms(xe, g).astype(BF16)
    r = lax.broadcasted_iota(jnp.int32, (ext, 1), 0)
    valid = jnp.logical_and(jnp.logical_or(r >= SSDIN_HALO, i > 0),
                            jnp.logical_or(r < SSDIN_HALO + rows, i < last))
    first_tap = SSDIN_HALO - SSD_CONV // 2
    nchunks = cdim // SSDIN_COLS
    tiles = SSDIN_COLS // V7X_LANES

    def project(ci):
        xbc = jnp.dot(u, wxbc_ref[:, ci * SSDIN_COLS:(ci + 1) * SSDIN_COLS], preferred_element_type=F32)
        xbc = jnp.where(valid, xbc, 0.0)
        for j in range(tiles):
            xs_ref[ci * tiles + j] = xbc[:, j * V7X_LANES:(j + 1) * V7X_LANES]

    def taps(ci):
        for j in range(ci * tiles, (ci + 1) * tiles):
            cs = slice(j * V7X_LANES, (j + 1) * V7X_LANES)
            for r0 in range(0, rows, SSDIN_CONV_ROWS):
                acc = jnp.broadcast_to(cb_ref[:, cs], (SSDIN_CONV_ROWS, V7X_LANES))
                for k in range(SSD_CONV):
                    a0 = r0 + first_tap + k
                    acc = acc + cw_ref[k:k + 1, cs] * xs_ref[j, a0:a0 + SSDIN_CONV_ROWS, :]
                xbc_ref[r0:r0 + SSDIN_CONV_ROWS, cs] = _silu(acc).astype(BF16)

    um = _rms(xm_ref[...], g).astype(BF16)
    side = []
    for c0 in range(0, zs_ref.shape[-1], SSDIN_COLS):
        def z_piece(c0=c0):
            z = jnp.dot(um, wz_ref[:, c0:c0 + SSDIN_COLS], preferred_element_type=F32)
            zs_ref[:, c0:c0 + SSDIN_COLS] = _silu(z).astype(BF16)
        side.append(z_piece)
    for c0 in range(0, gb_ref.shape[-1], SSDIN_COLS):
        def g_piece(c0=c0):
            gt = jnp.dot(um, wgb_ref[:, c0:c0 + SSDIN_COLS], preferred_element_type=F32)
            gb_ref[:, c0:c0 + SSDIN_COLS] = jax.nn.sigmoid(gt + bgb_ref[:, c0:c0 + SSDIN_COLS]).astype(BF16)
        side.append(g_piece)

    def dt_piece():
        dtx_ref[...] = _softplus(jnp.dot(um, wdt_ref[...], preferred_element_type=F32) + dtb_ref[...])
    side.append(dt_piece)

    project(0)
    for ci in range(nchunks):
        if ci + 1 < nchunks:
            project(ci + 1)
        if ci < len(side):
            side[ci]()
        taps(ci)
    for piece in side[nchunks:]:
        piece()


def _ssdin(x3, p, layer):
    b, l, d = x3.shape
    rows = SSDIN_ROWS
    inner = p["ssd_norm"].shape[-1]
    cdim = p["ssd_conv_w"].shape[-1]
    assert l % rows == 0 and rows % SSDIN_HALO == 0
    hb = rows // SSDIN_HALO
    nhb = l // SSDIN_HALO
    tok = lambda w: pl.BlockSpec((None, rows, w), lambda bi, i: (bi, i, 0))
    return pl.pallas_call(
        functools.partial(_ssdin_kernel, rows=rows),
        grid=(b, l // rows),
        in_specs=[
            tok(d),
            pl.BlockSpec((None, SSDIN_HALO, d), lambda bi, i: (bi, jnp.maximum(i * hb - 1, 0), 0)),
            pl.BlockSpec((None, SSDIN_HALO, d), lambda bi, i: (bi, jnp.minimum((i + 1) * hb, nhb - 1), 0)),
            _const_spec((1, d), layer),
            _layer_block((d, inner), layer, 1),
            _layer_block((d, cdim), layer, 1),
            _const_spec((d, V7X_LANES), layer),
            _const_spec((1, V7X_LANES), layer),
            _layer_block((d, d), layer, 1),
            _layer_block((1, d), layer, 1),
            _const_spec((SSD_CONV, cdim), layer),
            _const_spec((1, cdim), layer),
        ],
        out_specs=[tok(inner), tok(cdim), tok(V7X_LANES), tok(d)],
        out_shape=[jax.ShapeDtypeStruct((b, l, inner), BF16),
                   jax.ShapeDtypeStruct((b, l, cdim), BF16),
                   jax.ShapeDtypeStruct((b, l, V7X_LANES), F32),
                   jax.ShapeDtypeStruct((b, l, d), BF16)],
        scratch_shapes=[pltpu.VMEM((cdim // V7X_LANES, rows + 2 * SSDIN_HALO, V7X_LANES), F32)],
        compiler_params=pltpu.CompilerParams(
            dimension_semantics=("parallel", "parallel"), vmem_limit_bytes=VMEM_LIMIT),
        name="ssd_in",
    )(x3, x3, x3, p["mix_norm"], p["w_main"], p["w_main"], p["w_dt"], p["dt_bias"], p["w_gate"],
      p["gate_bias"], p["ssd_conv_w"], p["ssd_conv_b"])


def _chunk_stats(dtx, avec, ltri):
    a = dtx * avec
    hi, mid, lo = _split3(a)
    c = jnp.dot(ltri, jnp.concatenate([hi, mid, lo], axis=1), preferred_element_type=F32)
    n = a.shape[1]
    cum = c[:, :n] + c[:, n:2 * n] + c[:, 2 * n:]
    return a, cum, cum[SSD_CHUNK - 1:SSD_CHUNK, :]


def _expand(v, e_ref):
    hi, lo = _split2(v)
    return jnp.dot(jnp.concatenate([hi, lo], axis=1), e_ref[...], preferred_element_type=F32)


def _ssd_fwd_kernel(xs_ref, b_ref, dtx_ref, alog_ref, ltri_ref, e_ref, pf_ref, st_ref, *, heads):
    @pl.when(pl.program_id(1) == 0)
    def _():
        st_ref[...] = jnp.zeros_like(st_ref)

    lane = lax.broadcasted_iota(jnp.int32, (1, V7X_LANES), 1)
    avec = -jnp.exp(alog_ref[...])
    gw = st_ref.shape[1] // SSD_GROUPS
    in_fwd_copy = jnp.logical_and(lane >= 2 * heads, lane < 3 * heads)
    for k in range(SSD_CHUNKS_PER_STEP):
        rs = slice(k * SSD_CHUNK, (k + 1) * SSD_CHUNK)
        pf_ref[k] = st_ref[...].astype(BF16)
        dtx = dtx_ref[rs, :]
        _, cum, tot = _chunk_stats(dtx, avec, ltri_ref[...])
        w = jnp.where(in_fwd_copy, dtx * jnp.exp(tot - cum), 0.0)
        dec = jnp.where(in_fwd_copy, jnp.exp(tot), 0.0)
        wx = _expand(jnp.concatenate([w, jnp.broadcast_to(dec, (BF16_SUBLANES, V7X_LANES))], axis=0), e_ref)
        xw = (xs_ref[rs, :].astype(F32) * wx[:SSD_CHUNK]).astype(BF16)
        decx = wx[SSD_CHUNK:SSD_CHUNK + 1]
        for g in range(SSD_GROUPS):
            cs = slice(g * gw, (g + 1) * gw)
            bg = b_ref[rs, g * SSD_STATE:(g + 1) * SSD_STATE]
            s_t = lax.dot_general(bg, xw[:, cs], (((0,), (0,)), ((), ())), preferred_element_type=F32)
            st_ref[:, cs] = st_ref[:, cs] * decx[:, cs] + s_t


def _ssd_fwd(xbc, dtx, p, layer, consts):
    b, l, cdim = xbc.shape
    inner = p["ssd_norm"].shape[-1]
    heads = inner // SSD_HEAD_DIM
    rows = SSD_CHUNKS_PER_STEP * SSD_CHUNK
    nc = l // SSD_CHUNK
    assert l % rows == 0 and 4 * heads == V7X_LANES
    bc = SSD_GROUPS * SSD_STATE
    return pl.pallas_call(
        functools.partial(_ssd_fwd_kernel, heads=heads),
        grid=(b, l // rows),
        in_specs=[
            pl.BlockSpec((None, rows, inner), lambda bi, j: (bi, j, 0)),
            pl.BlockSpec((None, rows, bc), lambda bi, j: (bi, j, inner // bc)),
            pl.BlockSpec((None, rows, V7X_LANES), lambda bi, j: (bi, j, 0)),
            _const_spec((1, V7X_LANES), layer),
            _const_spec((SSD_CHUNK, SSD_CHUNK)),
            _const_spec((2 * V7X_LANES, inner)),
        ],
        out_specs=pl.BlockSpec((None, SSD_CHUNKS_PER_STEP, SSD_STATE, inner), lambda bi, j: (bi, j, 0, 0)),
        out_shape=jax.ShapeDtypeStruct((b, nc, SSD_STATE, inner), BF16),
        scratch_shapes=[pltpu.VMEM((SSD_STATE, inner), F32)],
        compiler_params=pltpu.CompilerParams(
            dimension_semantics=("parallel", "arbitrary"), vmem_limit_bytes=VMEM_LIMIT),
        name="ssd_fwd",
    )(xbc, xbc, dtx, p["a_log"], consts["ltri"], consts["e_fwd"])


def _ssd_out_kernel(xs_ref, b_ref, c_ref, zs_ref, dtx_ref, pf_ref, yag_ref, gb_ref, x_ref,
                    alog_ref, dx_ref, ng_ref, wso_ref, wo_ref, ltri_ref, ui_ref, e_ref,
                    o_ref, st_ref, vn_ref, *, heads):
    @pl.when(pl.program_id(1) == 0)
    def _():
        st_ref[...] = jnp.zeros_like(st_ref)

    inner = st_ref.shape[1]
    gw = inner // SSD_GROUPS
    hpg = heads // SSD_GROUPS
    lane = lax.broadcasted_iota(jnp.int32, (1, V7X_LANES), 1)
    avec = -jnp.exp(alog_ref[...])
    ri = lax.broadcasted_iota(jnp.int32, (SSD_CHUNK, SSD_CHUNK), 0)
    ci = lax.broadcasted_iota(jnp.int32, (SSD_CHUNK, SSD_CHUNK), 1)
    below = ci < ri
    above = ci > ri
    br = lax.broadcasted_iota(jnp.int32, (hpg * SSD_CHUNK, gw), 0) // SSD_CHUNK
    bcl = lax.broadcasted_iota(jnp.int32, (hpg * SSD_CHUNK, gw), 1) // SSD_HEAD_DIM
    blockdiag = br == bcl

    for k in reversed(range(SSD_CHUNKS_PER_STEP)):
        rs = slice(k * SSD_CHUNK, (k + 1) * SSD_CHUNK)
        dtx = dtx_ref[rs, :]
        a, cum, tot = _chunk_stats(dtx, avec, ltri_ref[...])
        ecum = cum - a
        yh, ym, yl = _split3(jnp.where(lane < 2 * heads, a, dtx))
        t3 = lax.dot_general(jnp.concatenate([yh, ym, yl], axis=1), ui_ref[...],
                             (((0,), (0,)), ((), ())), preferred_element_type=F32)
        tt = t3[:V7X_LANES] + t3[V7X_LANES:2 * V7X_LANES] + t3[2 * V7X_LANES:]
        dt_f_t = tt[2 * heads:3 * heads, SSD_CHUNK:]
        dt_b_t = tt[3 * heads:4 * heads, SSD_CHUNK:]
        src_f_t = tt[0:heads, :SSD_CHUNK] - jnp.log(dt_f_t)
        src_b_t = tt[heads:2 * heads, :SSD_CHUNK] - tt[heads:2 * heads, SSD_CHUNK:] + jnp.log(dt_b_t)
        diag_t = jnp.log(dt_f_t + dt_b_t)

        scal = jnp.where(lane < heads, jnp.exp(cum),
                         jnp.where(lane < 2 * heads, jnp.exp(tot - ecum),
                                   jnp.where(lane >= 3 * heads, dtx * jnp.exp(ecum), 0.0)))
        dec = jnp.where(lane >= 3 * heads, jnp.exp(tot), 0.0)
        ex = _expand(jnp.concatenate([scal, jnp.broadcast_to(dec, (BF16_SUBLANES, V7X_LANES))], axis=0), e_ref)
        e_f = ex[:SSD_CHUNK, 0:inner]
        e_b = ex[:SSD_CHUNK, inner:2 * inner]
        w_b = ex[:SSD_CHUNK, 2 * inner:]
        dec_b = ex[SSD_CHUNK:SSD_CHUNK + 1, 2 * inner:]

        xs = xs_ref[rs, :]
        xw = (xs.astype(F32) * w_b).astype(BF16)
        for g in range(SSD_GROUPS):
            cs = slice(g * gw, (g + 1) * gw)
            ns = slice(g * SSD_STATE, (g + 1) * SSD_STATE)
            bg = b_ref[rs, ns]
            cg = c_ref[rs, ns]
            s = lax.dot_general(cg, bg, (((1,), (1,)), ((), ())), preferred_element_type=F32)
            sg = []
            for hh in range(hpg):
                h = g * hpg + hh
                arg = jnp.where(below, cum[:, h:h + 1] - src_f_t[h:h + 1, :],
                                jnp.where(above, src_b_t[h:h + 1, :] - ecum[:, heads + h:heads + h + 1],
                                          diag_t[h:h + 1, :]))
                sg.append((s * jnp.exp(arg)).astype(BF16))
            xg = xs[:, cs]
            xbd = jnp.where(blockdiag, jnp.concatenate([xg] * hpg, axis=0), jnp.zeros((), BF16))
            y = jnp.dot(jnp.concatenate(sg, axis=1), xbd, preferred_element_type=F32)
            y = y + e_f[:, cs] * jnp.dot(cg, pf_ref[k, :, cs], preferred_element_type=F32)
            y = y + e_b[:, cs] * jnp.dot(cg, st_ref[:, cs].astype(BF16), preferred_element_type=F32)
            y = y + xg.astype(F32) * dx_ref[:, cs]
            s_t = lax.dot_general(bg, xw[:, cs], (((0,), (0,)), ((), ())), preferred_element_type=F32)
            st_ref[:, cs] = st_ref[:, cs] * dec_b[:, cs] + s_t
            v = y * zs_ref[rs, cs].astype(F32)
            vn = v * lax.rsqrt(jnp.mean(v * v, axis=-1, keepdims=True) + EPS) * ng_ref[:, cs]
            vn_ref[rs, cs] = vn.astype(BF16)

    yb = jnp.dot(vn_ref[...], wso_ref[...], preferred_element_type=F32)
    mix = (yag_ref[...].astype(F32) + gb_ref[...].astype(F32) * yb).astype(BF16)
    o_ref[...] = x_ref[...] + jnp.dot(mix, wo_ref[...], preferred_element_type=F32)


def _ssd_out(xbc, zs, dtx, pf, yag, gb, x3, p, layer, consts):
    b, l, d = x3.shape
    inner = p["ssd_norm"].shape[-1]
    heads = inner // SSD_HEAD_DIM
    rows = SSD_CHUNKS_PER_STEP * SSD_CHUNK
    nblk = l // rows
    bc = SSD_GROUPS * SSD_STATE
    rev = lambda bi, j: (bi, nblk - 1 - j, 0)
    return pl.pallas_call(
        functools.partial(_ssd_out_kernel, heads=heads),
        grid=(b, nblk),
        in_specs=[
            pl.BlockSpec((None, rows, inner), rev),
            pl.BlockSpec((None, rows, bc), lambda bi, j: (bi, nblk - 1 - j, inner // bc)),
            pl.BlockSpec((None, rows, bc), lambda bi, j: (bi, nblk - 1 - j, inner // bc + 1)),
            pl.BlockSpec((None, rows, inner), rev),
            pl.BlockSpec((None, rows, V7X_LANES), rev),
            pl.BlockSpec((None, SSD_CHUNKS_PER_STEP, SSD_STATE, inner), lambda bi, j: (bi, nblk - 1 - j, 0, 0)),
            pl.BlockSpec((None, rows, d), rev),
            pl.BlockSpec((None, rows, d), rev),
            pl.BlockSpec((None, rows, d), rev),
            _const_spec((1, V7X_LANES), layer),
            _const_spec((1, inner), layer),
            _const_spec((1, inner), layer),
            _const_spec((inner, d), layer),
            _const_spec((d, d), layer),
            _const_spec((SSD_CHUNK, SSD_CHUNK)),
            _const_spec((SSD_CHUNK, 2 * SSD_CHUNK)),
            _const_spec((2 * V7X_LANES, 3 * inner)),
        ],
        out_specs=pl.BlockSpec((None, rows, d), rev),
        out_shape=jax.ShapeDtypeStruct((b, l, d), F32),
        scratch_shapes=[pltpu.VMEM((SSD_STATE, inner), F32), pltpu.VMEM((rows, inner), BF16)],
        compiler_params=pltpu.CompilerParams(
            dimension_semantics=("parallel", "arbitrary"), vmem_limit_bytes=VMEM_LIMIT),
        name="ssd_out",
    )(xbc, xbc, xbc, zs, dtx, pf, yag, gb, x3, p["a_log"], p["d_skip"], p["ssd_norm"],
      p["ssd_w_out"], p["w_o"], consts["ltri"], consts["ui"], consts["e_out"])


def _scan_constants(inner):
    heads = inner // SSD_HEAD_DIM
    t = jnp.arange(SSD_CHUNK)
    ltri = (t[None, :] <= t[:, None]).astype(BF16)
    ui = jnp.concatenate([(t[:, None] <= t[None, :]).astype(BF16),
                          jnp.eye(SSD_CHUNK, dtype=BF16)], axis=1)
    lane = jnp.arange(V7X_LANES)
    col_head = jnp.arange(inner) // SSD_HEAD_DIM

    def expander(lane_groups):
        blocks = [(lane[:, None] == g * heads + col_head[None, :]).astype(BF16) for g in lane_groups]
        e = jnp.concatenate(blocks, axis=1)
        return jnp.concatenate([e, e], axis=0)

    return {"ltri": ltri, "ui": ui, "e_fwd": expander([2]), "e_out": expander([0, 1, 3])}


def kernel(x_prompt, x_sample, ffn1_norm, ffn1_w_up, ffn1_w_down, mix_norm, w_in, gate_bias, conf_dw_w, conf_dw_b, conf_ln_g, conf_ln_b, conf_w_out, ssd_conv_w, ssd_conv_b, ssd_dt_bias, ssd_A_log, ssd_D, ssd_norm, ssd_w_out, w_o, ffn2_norm, ffn2_w_up, ffn2_w_down, final_norm):
    depth, d = ffn1_norm.shape
    width = conf_dw_w.shape[-1]
    inner = ssd_norm.shape[-1]
    cdim = ssd_conv_w.shape[-1]
    heads = inner // SSD_HEAD_DIM
    n_main = 2 * width + inner + cdim
    assert x_prompt.shape[1] == x_sample.shape[1]

    row = lambda v: v.reshape(depth, 1, -1).astype(F32)
    w_dt = w_in[:, :, n_main:n_main + 2 * heads]
    p = {
        "mix_norm": row(mix_norm),
        "w_main": w_in[:, :, :n_main].astype(BF16),
        "w_gate": w_in[:, :, n_main + 2 * heads:].astype(BF16),
        "w_dt": jnp.concatenate([w_dt, w_dt], axis=-1).astype(BF16),
        "dt_bias": jnp.tile(ssd_dt_bias.reshape(depth, 1, 2 * heads), (1, 1, 2)).astype(F32),
        "gate_bias": row(gate_bias),
        "conf_dw_w": conf_dw_w.astype(F32), "conf_dw_b": row(conf_dw_b),
        "conf_ln_g": row(conf_ln_g), "conf_ln_b": row(conf_ln_b),
        "conf_w_out": conf_w_out.astype(BF16),
        "ssd_conv_w": ssd_conv_w.astype(F32), "ssd_conv_b": row(ssd_conv_b),
        "a_log": jnp.tile(ssd_A_log.reshape(depth, 1, 2 * heads), (1, 1, 2)).astype(F32),
        "d_skip": jnp.repeat(ssd_D, SSD_HEAD_DIM, axis=-1).reshape(depth, 1, inner).astype(F32),
        "ssd_norm": row(ssd_norm),
        "ssd_w_out": ssd_w_out.astype(BF16),
        "w_o": w_o.astype(BF16),
    }
    f1n, f2n = row(ffn1_norm), row(ffn2_norm)
    f1u, f1d = ffn1_w_up.astype(BF16), ffn1_w_down.astype(BF16)
    f2u, f2d = ffn2_w_up.astype(BF16), ffn2_w_down.astype(BF16)
    fg = final_norm.reshape(1, d).astype(F32)
    consts = _scan_constants(inner)

    nb_prompt = x_prompt.shape[0]
    x3 = jnp.concatenate([x_prompt, x_sample], axis=0)
    b, l, _ = x3.shape
    for i in range(depth):
        x3 = _ffn(x3.reshape(b * l, d), f1n, f1u, f1d, fg, i, False).reshape(b, l, d)
        yag = _conf(x3, p, i)
        zs, xbc, dtx, gb = _ssdin(x3, p, i)
        pf = _ssd_fwd(xbc, dtx, p, i, consts)
        x3 = _ssd_out(xbc, zs, dtx, pf, yag, gb, x3, p, i, consts)
        x3 = _ffn(x3.reshape(b * l, d), f2n, f2u, f2d, fg, i, i == depth - 1).reshape(b, l, d)
    return (x3[:nb_prompt], x3[nb_prompt:])
```
